```python
import math
import jax, jax.numpy as jnp
from jax import lax
import numpy as np

D_MODEL = 2048
BATCH = 2
SEQ = 8192
DEPTH = 2

N_MIXERS = 2
N_A_LAYERS = (DEPTH + 1) // 2
N_B_LAYERS = DEPTH // 2
EPS = 1e-6

M_HEADS = 8
M_QK_DIM = D_MODEL // 2
M_V_DIM = D_MODEL
M_DK = M_QK_DIM // M_HEADS
M_DV = M_V_DIM // M_HEADS
M_PROJ = 2 * M_QK_DIM + 2 * M_V_DIM + 2 * M_HEADS
M_CHUNK = 64

R_WIDTH = D_MODEL
R_BLOCKS = 8
R_BLOCK_W = R_WIDTH // R_BLOCKS
R_CONV_W = 4
R_C = 8.0

D_FF = ((8 * D_MODEL // 3 + 255) // 256) * 256

kernel_name = "hybrid_mlstm_rglru_interleaved"


def rms_norm(x, g):
    xf = x.astype(jnp.float32)
    y = xf * lax.rsqrt(jnp.mean(xf * xf, axis=-1, keepdims=True) + EPS)
    return (y * g.astype(jnp.float32)).astype(x.dtype)


def swiglu(x, w_in, w_out):
    g, u = jnp.split(x @ w_in, 2, axis=-1)
    return (jax.nn.silu(g) * u) @ w_out


def mlstm_chunkwise(q, k, v, ig, lf):
    B, H, S, DK = q.shape
    DV = v.shape[-1]
    nc = S // M_CHUNK

    def to_chunks(t):
        return jnp.moveaxis(t.reshape(B, H, nc, M_CHUNK, *t.shape[3:]), 2, 0)

    qc, kc, vc, ic, fc = (to_chunks(t) for t in (q, k, v, ig, lf))
    causal = jnp.tril(jnp.ones((M_CHUNK, M_CHUNK), dtype=bool))

    def step(carry, xs):
        C, n, m = carry
        qb, kb, vb, ib, fb = xs
        b = jnp.cumsum(fb, axis=-1)
        dmat = b[..., :, None] - b[..., None, :] + ib[..., None, :]
        dmat = jnp.where(causal, dmat, -jnp.inf)
        m_inter = b + m[..., None]
        m_t = jnp.maximum(m_inter, jnp.max(dmat, axis=-1))
        w = jnp.exp(dmat - m_t[..., None])
        s = jnp.einsum('bhtd,bhsd->bhts', qb, kb) * w
        scale_inter = jnp.exp(m_inter - m_t)
        num = (jnp.einsum('bhts,bhsv->bhtv', s, vb)
               + scale_inter[..., None] * jnp.einsum('bhtd,bhdv->bhtv', qb, C))
        den = jnp.sum(s, axis=-1) + scale_inter * jnp.einsum('bhtd,bhd->bht', qb, n)
        h = num / jnp.maximum(jnp.abs(den), jnp.exp(-m_t))[..., None]
        b_last = b[..., -1]
        g = b_last[..., None] - b + ib
        m_new = jnp.maximum(b_last + m, jnp.max(g, axis=-1))
        wk = jnp.exp(g - m_new[..., None])
        decay = jnp.exp(b_last + m - m_new)
        kw = kb * wk[..., None]
        C_new = decay[..., None, None] * C + jnp.einsum('bhsd,bhsv->bhdv', kw, vb)
        n_new = decay[..., None] * n + jnp.sum(kw, axis=-2)
        return (C_new, n_new, m_new), h

    init = (jnp.zeros((B, H, DK, DV), jnp.float32),
            jnp.zeros((B, H, DK), jnp.float32),
            jnp.zeros((B, H), jnp.float32))
    _, hc = lax.scan(step, init, (qc, kc, vc, ic, fc))
    return jnp.moveaxis(hc, 0, 2).reshape(B, H, S, DV)


def mlstm_mixer(x, w_in, b_if, head_norm, w_out):
    B, S, _ = x.shape
    proj = x @ w_in
    q, k, v, o, if_pre = jnp.split(
        proj, [M_QK_DIM, 2 * M_QK_DIM, 2 * M_QK_DIM + M_V_DIM, 2 * M_QK_DIM + 2 * M_V_DIM], axis=-1)

    def heads(t, d):
        return t.reshape(B, S, M_HEADS, d).transpose(0, 2, 1, 3).astype(jnp.float32)

    qh = heads(q, M_DK)
    kh = heads(k, M_DK) * (M_DK ** -0.5)
    vh = heads(v, M_DV)
    gates = (if_pre + b_if).astype(jnp.float32).reshape(B, S, 2, M_HEADS)
    ig = gates[:, :, 0].transpose(0, 2, 1)
    lf = jax.nn.log_sigmoid(gates[:, :, 1]).transpose(0, 2, 1)
    h = mlstm_chunkwise(qh, kh, vh, ig, lf)
    h = h * lax.rsqrt(jnp.mean(h * h, axis=-1, keepdims=True) + EPS)
    h = h.transpose(0, 2, 1, 3).reshape(B, S, M_V_DIM) * head_norm.astype(jnp.float32)
    h = h * jax.nn.sigmoid(o.astype(jnp.float32))
    return h.astype(x.dtype) @ w_out


def causal_depthwise_conv(x, w, b):
    C = x.shape[-1]
    y = lax.conv_general_dilated(
        x, w[:, None, :].astype(x.dtype), window_strides=(1,),
        padding=[(R_CONV_W - 1, 0)], dimension_numbers=('NWC', 'WIO', 'NWC'),
        feature_group_count=C)
    return y + b


def rglru_mixer(x, w_in, conv_w, conv_b, gate_w, gate_b, a_param, w_out):
    B, S, _ = x.shape
    gate_branch, rec = jnp.split(x @ w_in, 2, axis=-1)
    rec = causal_depthwise_conv(rec, conv_w, conv_b)
    xb = rec.reshape(B, S, R_BLOCKS, R_BLOCK_W)
    gates = jnp.einsum('bsgi,gio->bsgo', xb, gate_w) + gate_b
    r_pre, i_pre = jnp.split(gates.astype(jnp.float32), 2, axis=-1)
    r = jax.nn.sigmoid(r_pre).reshape(B, S, R_WIDTH)
    i = jax.nn.sigmoid(i_pre).reshape(B, S, R_WIDTH)
    log_a = R_C * r * jax.nn.log_sigmoid(a_param.astype(jnp.float32))
    a = jnp.exp(log_a)
    mult = jnp.sqrt(-jnp.expm1(2.0 * log_a))
    u = mult * (i * rec.astype(jnp.float32))

    def combine(left, right):
        a1, b1 = left
        a2, b2 = right
        return a2 * a1, a2 * b1 + b2

    _, h = lax.associative_scan(combine, (a, u), axis=1)
    y = jax.nn.gelu(gate_branch.astype(jnp.float32)) * h
    return y.astype(x.dtype) @ w_out


def setup_inputs(seed: int = 0) -> dict:
    key = jax.random.key(seed)
    ks = jax.random.split(key, 20)
    f32 = jnp.float32

    def nrm(k, shape, scale):
        return jax.random.normal(k, shape, f32) * scale

    x = jax.random.normal(ks[0], (BATCH, SEQ, D_MODEL), f32)
    norm_mix = 1.0 + nrm(ks[1], (DEPTH, D_MODEL), 0.02)
    norm_ffn = 1.0 + nrm(ks[2], (DEPTH, D_MODEL), 0.02)
    norm_final = 1.0 + nrm(ks[3], (D_MODEL,), 0.02)

    m_w_in = nrm(ks[4], (N_A_LAYERS, D_MODEL, M_PROJ), D_MODEL ** -0.5)
    kb1, kb2 = jax.random.split(ks[5])
    m_b_if = jnp.concatenate([nrm(kb1, (N_A_LAYERS, M_HEADS), 0.1),
                              3.0 + nrm(kb2, (N_A_LAYERS, M_HEADS), 0.1)], axis=-1)
    m_head_norm = 1.0 + nrm(ks[6], (N_A_LAYERS, M_V_DIM), 0.02)
    m_w_out = nrm(ks[7], (N_A_LAYERS, M_V_DIM, D_MODEL), M_V_DIM ** -0.5)

    r_w_in = nrm(ks[8], (N_B_LAYERS, D_MODEL, 2 * R_WIDTH), D_MODEL ** -0.5)
    r_conv_w = nrm(ks[9], (N_B_LAYERS, R_CONV_W, R_WIDTH), R_CONV_W ** -0.5)
    r_conv_b = nrm(ks[10], (N_B_LAYERS, R_WIDTH), 0.02)
    r_gate_w = nrm(ks[11], (N_B_LAYERS, R_BLOCKS, R_BLOCK_W, 2 * R_BLOCK_W), R_BLOCK_W ** -0.5)
    r_gate_b = nrm(ks[12], (N_B_LAYERS, R_BLOCKS, 2 * R_BLOCK_W), 0.02)
    a_c = jax.random.uniform(ks[13], (N_B_LAYERS, R_WIDTH), f32, 0.9, 0.999)
    a_base = a_c ** (1.0 / R_C)
    r_a_param = jnp.log(a_base) - jnp.log1p(-a_base)
    r_w_out = nrm(ks[14], (N_B_LAYERS, R_WIDTH, D_MODEL), R_WIDTH ** -0.5)

    ffn_w_in = nrm(ks[15], (DEPTH, D_MODEL, 2 * D_FF), D_MODEL ** -0.5)
    ffn_w_out = nrm(ks[16], (DEPTH, D_FF, D_MODEL), D_FF ** -0.5)

    return {"x": x, "norm_mix": norm_mix, "norm_ffn": norm_ffn, "norm_final": norm_final,
            "m_w_in": m_w_in, "m_b_if": m_b_if, "m_head_norm": m_head_norm, "m_w_out": m_w_out,
            "r_w_in": r_w_in, "r_conv_w": r_conv_w, "r_conv_b": r_conv_b, "r_gate_w": r_gate_w,
            "r_gate_b": r_gate_b, "r_a_param": r_a_param, "r_w_out": r_w_out,
            "ffn_w_in": ffn_w_in, "ffn_w_out": ffn_w_out}


def reference(x, norm_mix, norm_ffn, norm_final, m_w_in, m_b_if, m_head_norm, m_w_out,
              r_w_in, r_conv_w, r_conv_b, r_gate_w, r_gate_b, r_a_param, r_w_out,
              ffn_w_in, ffn_w_out):
    h = x
    for layer in range(DEPTH):
        hn = rms_norm(h, norm_mix[layer])
        j = layer // N_MIXERS
        if layer % N_MIXERS == 0:
            mix = mlstm_mixer(hn, m_w_in[j], m_b_if[j], m_head_norm[j], m_w_out[j])
        else:
            mix = rglru_mixer(hn, r_w_in[j], r_conv_w[j], r_conv_b[j], r_gate_w[j],
                              r_gate_b[j], r_a_param[j], r_w_out[j])
        h = h + mix
        h = h + swiglu(rms_norm(h, norm_ffn[layer]), ffn_w_in[layer], ffn_w_out[layer])
    return rms_norm(h, norm_final)
```

```python
import functools

import jax
import jax.numpy as jnp
from jax import lax
from jax.experimental import pallas as pl
from jax.experimental.pallas import tpu as pltpu

F32 = jnp.float32
BF16 = jnp.bfloat16

EPS = 1e-6
M_HEADS = 8
R_BLOCKS = 8
R_CONV_W = 4
R_C = 8.0
LANES = 128
SUBLANES = 8
VMEM_LIMIT = 56 * 1024 * 1024

MLSTM_CHUNK = 256
ROW_CHUNK = 256


def _cparams(sem):
    return pltpu.CompilerParams(dimension_semantics=sem, vmem_limit_bytes=VMEM_LIMIT)


def _rms_rows(x, g):
    ms = jnp.mean(x * x, axis=-1, keepdims=True)
    return x * lax.rsqrt(ms + EPS) * g


def _norm_proj_kernel(x_ref, g_ref, w_ref, o_ref, xn_ref):
    @pl.when(pl.program_id(1) == 0)
    def _():
        def body(r, c):
            rows = pl.ds(pl.multiple_of(r * ROW_CHUNK, ROW_CHUNK), ROW_CHUNK)
            xn_ref[rows, :] = _rms_rows(x_ref[rows, :], g_ref[...]).astype(BF16)
            return c
        lax.fori_loop(0, x_ref.shape[0] // ROW_CHUNK, body, 0)

    o_ref[...] = jnp.dot(xn_ref[...], w_ref[...], preferred_element_type=F32)


def _norm_proj_gates_kernel(x_ref, g_ref, w_ref, wg_ref, o_ref, og_ref, xn_ref):
    @pl.when(pl.program_id(1) == 0)
    def _():
        def body(r, c):
            rows = pl.ds(pl.multiple_of(r * ROW_CHUNK, ROW_CHUNK), ROW_CHUNK)
            xn_ref[rows, :] = _rms_rows(x_ref[rows, :], g_ref[...]).astype(BF16)
            return c
        lax.fori_loop(0, x_ref.shape[0] // ROW_CHUNK, body, 0)
        og_ref[...] = jnp.dot(xn_ref[...], wg_ref[...], preferred_element_type=F32)

    o_ref[...] = jnp.dot(xn_ref[...], w_ref[...], preferred_element_type=F32)


def _norm_proj(x, gain, w, w_gates=None, *, tm=1024, tn=1024):
    T, D = x.shape
    N = w.shape[1]
    grid = (T // tm, N // tn)
    x_spec = pl.BlockSpec((tm, D), lambda i, j: (i, 0))
    g_spec = pl.BlockSpec((1, D), lambda i, j: (0, 0))
    w_spec = pl.BlockSpec((D, tn), lambda i, j: (0, j))
    o_spec = pl.BlockSpec((tm, tn), lambda i, j: (i, j))
    scratch = [pltpu.VMEM((tm, D), BF16)]
    if w_gates is None:
        return pl.pallas_call(
            _norm_proj_kernel,
            grid=grid,
            in_specs=[x_spec, g_spec, w_spec],
            out_specs=o_spec,
            out_shape=jax.ShapeDtypeStruct((T, N), F32),
            scratch_shapes=scratch,
            compiler_params=_cparams(("arbitrary", "arbitrary")),
            name="norm_proj",
        )(x, gain, w)
    NG = w_gates.shape[1]
    return pl.pallas_call(
        _norm_proj_gates_kernel,
        grid=grid,
        in_specs=[x_spec, g_spec, w_spec, pl.BlockSpec((D, NG), lambda i, j: (0, 0))],
        out_specs=[o_spec, pl.BlockSpec((tm, NG), lambda i, j: (i, 0))],
        out_shape=[jax.ShapeDtypeStruct((T, N), F32), jax.ShapeDtypeStruct((T, NG), F32)],
        scratch_shapes=scratch,
        compiler_params=_cparams(("arbitrary", "arbitrary")),
        name="norm_proj_gates",
    )(x, gain, w, w_gates)


def _prefix_rows(x, op, fill):
    L = x.shape[0]
    row = lax.broadcasted_iota(jnp.int32, x.shape, 0)
    d = 1
    while d < L:
        shifted = pltpu.roll(x, d, axis=0)
        x = op(x, jnp.where(row >= d, shifted, fill))
        d *= 2
    return x


def _mlstm_kernel(q_ref, k_ref, v_ref, o_ref, gi_ref, gf_ref, bi_ref, bf_ref, hn_ref,
                  out_ref, c_ref, n_ref, m_ref, *, heads, dk, dv):
    L = q_ref.shape[0]
    scale = dk ** -0.5

    @pl.when(pl.program_id(1) == 0)
    def _():
        c_ref[...] = jnp.zeros_like(c_ref)
        n_ref[...] = jnp.zeros_like(n_ref)
        m_ref[...] = jnp.zeros_like(m_ref)

    gi = gi_ref[...] + bi_ref[...]
    lf = jax.nn.log_sigmoid(gf_ref[...] + bf_ref[...])
    b = _prefix_rows(lf, jnp.add, 0.0)
    m_prev = m_ref[...]
    src = gi - b
    m_inter = b + m_prev
    m_t = jnp.maximum(m_inter, b + _prefix_rows(src, jnp.maximum, -jnp.inf))
    scale_inter = jnp.exp(m_inter - m_t)
    tgt = b - m_t
    floor = jnp.exp(-m_t)
    b_last = b[L - 1:L, :]
    g = b_last - b + gi
    m_new = jnp.maximum(b_last + m_prev, jnp.max(g, axis=0, keepdims=True))
    wk = jnp.exp(g - m_new)
    decay = jnp.exp(b_last + m_prev - m_new)
    m_ref[...] = m_new
    src_t = src.T

    row = lax.broadcasted_iota(jnp.int32, (L, L), 0)
    col = lax.broadcasted_iota(jnp.int32, (L, L), 1)
    causal = col <= row

    for h in range(heads):
        ksl = slice(h * dk, (h + 1) * dk)
        vsl = slice(h * dv, (h + 1) * dv)
        q = q_ref[:, ksl]
        ks = k_ref[:, ksl] * scale
        qb = q.astype(BF16)
        vb = v_ref[:, vsl].astype(BF16)
        qk = lax.dot_general(qb, ks.astype(BF16), (((1,), (1,)), ((), ())),
                             preferred_element_type=F32)
        logw = tgt[:, h:h + 1] + src_t[h:h + 1, :]
        s = qk * jnp.exp(jnp.where(causal, logw, -jnp.inf))
        si = scale_inter[:, h:h + 1]
        num = (jnp.dot(s.astype(BF16), vb, preferred_element_type=F32)
               + si * jnp.dot(qb, c_ref[h].astype(BF16), preferred_element_type=F32))
        den = (jnp.sum(s, axis=-1, keepdims=True)
               + si * jnp.sum(q * n_ref[h], axis=-1, keepdims=True))
        hh = num / jnp.maximum(jnp.abs(den), floor[:, h:h + 1])
        ms = jnp.mean(hh * hh, axis=-1, keepdims=True)
        y = hh * lax.rsqrt(ms + EPS) * hn_ref[:, vsl] * jax.nn.sigmoid(o_ref[:, vsl])
        out_ref[:, vsl] = y.astype(out_ref.dtype)

        kw = ks * wk[:, h:h + 1]
        dc = lax.dot_general(kw.astype(BF16), vb, (((0,), (0,)), ((), ())),
                             preferred_element_type=F32)
        dec = decay[:, h:h + 1]
        c_ref[h] = dec * c_ref[h] + dc
        n_ref[h] = dec * n_ref[h] + jnp.sum(kw, axis=0, keepdims=True)


def _mlstm(proj, gates, b_i, b_f, head_norm, *, batch, seq, heads, dk, dv):
    T = proj.shape[0]
    L = MLSTM_CHUNK
    nc = seq // L
    qk_w, v_w = heads * dk, heads * dv
    v_blk = (2 * qk_w) // v_w
    tok = lambda bi, c: bi * nc + c
    kern = functools.partial(_mlstm_kernel, heads=heads, dk=dk, dv=dv)
    return pl.pallas_call(
        kern,
        grid=(batch, nc),
        in_specs=[
            pl.BlockSpec((L, qk_w), lambda bi, c: (tok(bi, c), 0)),
            pl.BlockSpec((L, qk_w), lambda bi, c: (tok(bi, c), 1)),
            pl.BlockSpec((L, v_w), lambda bi, c: (tok(bi, c), v_blk)),
            pl.BlockSpec((L, v_w), lambda bi, c: (tok(bi, c), v_blk + 1)),
            pl.BlockSpec((L, LANES), lambda bi, c: (tok(bi, c), 0)),
            pl.BlockSpec((L, LANES), lambda bi, c: (tok(bi, c), 1)),
            pl.BlockSpec((1, LANES), lambda bi, c: (0, 0)),
            pl.BlockSpec((1, LANES), lambda bi, c: (0, 0)),
            pl.BlockSpec((1, v_w), lambda bi, c: (0, 0)),
        ],
        out_specs=pl.BlockSpec((L, v_w), lambda bi, c: (tok(bi, c), 0)),
        out_shape=jax.ShapeDtypeStruct((T, v_w), BF16),
        scratch_shapes=[pltpu.VMEM((heads, dk, dv), F32),
                        pltpu.VMEM((heads, 1, dk), F32),
                        pltpu.VMEM((1, LANES), F32)],
        compiler_params=_cparams(("arbitrary", "arbitrary")),
        name="mlstm",
    )(proj, proj, proj, proj, gates, gates, b_i, b_f, head_norm)


def _proj_residual_kernel(a_ref, w_ref, r_ref, o_ref):
    o_ref[...] = r_ref[...] + jnp.dot(a_ref[...], w_ref[...], preferred_element_type=F32)


def _proj_residual(a, w, resid, *, tm=512):
    T, K = a.shape
    N = w.shape[1]
    return pl.pallas_call(
        _proj_residual_kernel,
        grid=(T // tm,),
        in_specs=[pl.BlockSpec((tm, K), lambda i: (i, 0)),
                  pl.BlockSpec((K, N), lambda i: (0, 0)),
                  pl.BlockSpec((tm, N), lambda i: (i, 0))],
        out_specs=pl.BlockSpec((tm, N), lambda i: (i, 0)),
        out_shape=jax.ShapeDtypeStruct((T, N), F32),
        compiler_params=_cparams(("arbitrary",)),
        name="proj_residual",
    )(a, w, resid)


def _ffn_kernel(x_ref, g_ref, wg_ref, wu_ref, wo_ref, gf_ref, o_ref, xn_ref, *, final_norm):
    j = pl.program_id(1)
    n_chunks = x_ref.shape[0] // ROW_CHUNK

    @pl.when(j == 0)
    def _():
        def body(r, c):
            rows = pl.ds(pl.multiple_of(r * ROW_CHUNK, ROW_CHUNK), ROW_CHUNK)
            x = x_ref[rows, :]
            xn_ref[rows, :] = _rms_rows(x, g_ref[...]).astype(BF16)
            o_ref[rows, :] = x
            return c
        lax.fori_loop(0, n_chunks, body, 0)

    xn = xn_ref[...]
    gate = jnp.dot(xn, wg_ref[...], preferred_element_type=F32)
    up = jnp.dot(xn, wu_ref[...], preferred_element_type=F32)
    act = (gate * jax.nn.sigmoid(gate) * up).astype(BF16)
    o_ref[...] += jnp.dot(act, wo_ref[...], preferred_element_type=F32)

    if final_norm:
        @pl.when(j == pl.num_programs(1) - 1)
        def _():
            def body(r, c):
                rows = pl.ds(pl.multiple_of(r * ROW_CHUNK, ROW_CHUNK), ROW_CHUNK)
                o_ref[rows, :] = _rms_rows(o_ref[rows, :], gf_ref[...])
                return c
            lax.fori_loop(0, n_chunks, body, 0)


def _ffn(x, gain, w_in, w_out, gain_final, *, final_norm, tm=512, tf=512):
    T, D = x.shape
    d_ff = w_out.shape[0]
    nj = d_ff // tf
    kern = functools.partial(_ffn_kernel, final_norm=final_norm)
    return pl.pallas_call(
        kern,
        grid=(T // tm, nj),
        in_specs=[pl.BlockSpec((tm, D), lambda i, j: (i, 0)),
                  pl.BlockSpec((1, D), lambda i, j: (0, 0)),
                  pl.BlockSpec((D, tf), lambda i, j: (0, j)),
                  pl.BlockSpec((D, tf), lambda i, j: (0, nj + j)),
                  pl.BlockSpec((tf, D), lambda i, j: (j, 0)),
                  pl.BlockSpec((1, D), lambda i, j: (0, 0))],
        out_specs=pl.BlockSpec((tm, D), lambda i, j: (i, 0)),
        out_shape=jax.ShapeDtypeStruct((T, D), F32),
        scratch_shapes=[pltpu.VMEM((tm, D), BF16)],
        compiler_params=_cparams(("arbitrary", "arbitrary")),
        name="ffn_final" if final_norm else "ffn",
    )(x, gain, w_in, w_in, w_out, gain_final)


def _rglru_kernel(gate_ref, rec_ref, cw_ref, cb_ref, gw_ref, gb_ref, ap_ref, y_ref,
                  buf_ref, a_ref, u_ref, h_ref, *, blocks, conv_w):
    ts, W = rec_ref.shape
    bw = W // blocks
    pad = SUBLANES

    @pl.when(pl.program_id(1) == 0)
    def _():
        buf_ref[0:pad, :] = jnp.zeros((pad, W), F32)
        h_ref[...] = jnp.zeros_like(h_ref)

    buf_ref[pad:pad + ts, :] = rec_ref[...]

    for blk in range(blocks):
        cs = slice(blk * bw, (blk + 1) * bw)
        conv = cb_ref[:, cs] + cw_ref[conv_w - 1:conv_w, cs] * buf_ref[pad:pad + ts, cs]
        for tap in range(conv_w - 1):
            off = pad - (conv_w - 1) + tap
            conv = conv + cw_ref[tap:tap + 1, cs] * buf_ref[off:off + ts, cs]
        pre = jnp.dot(conv.astype(BF16), gw_ref[blk], preferred_element_type=F32) + gb_ref[blk]
        r = jax.nn.sigmoid(pre[:, :bw])
        i = jax.nn.sigmoid(pre[:, bw:])
        log_a = (R_C * jax.nn.log_sigmoid(ap_ref[:, cs])) * r
        a = jnp.exp(log_a)
        a_ref[:, cs] = a
        u_ref[:, cs] = jnp.sqrt(jnp.tanh(-log_a) * (a * a + 1.0)) * (i * conv)

    buf_ref[0:pad, :] = buf_ref[ts:ts + pad, :]

    row = lax.broadcasted_iota(jnp.int32, (SUBLANES, W), 0)

    def scan_rows(r, h_prev):
        rows = pl.ds(pl.multiple_of(r * SUBLANES, SUBLANES), SUBLANES)
        a = a_ref[rows, :]
        u = u_ref[rows, :]
        d = 1
        while d < SUBLANES:
            keep = row >= d
            u = jnp.where(keep, a * pltpu.roll(u, d, axis=0) + u, u)
            a = jnp.where(keep, a * pltpu.roll(a, d, axis=0), a)
            d *= 2
        h = a * h_prev + u
        u_ref[rows, :] = h
        return h[SUBLANES - 1:SUBLANES, :]

    h_ref[...] = lax.fori_loop(0, ts // SUBLANES, scan_rows, h_ref[...])

    for blk in range(blocks):
        cs = slice(blk * bw, (blk + 1) * bw)
        y_ref[:, cs] = (jax.nn.gelu(gate_ref[:, cs]) * u_ref[:, cs]).astype(y_ref.dtype)


def _rglru(proj, conv_w, conv_b, gate_w, gate_b, a_param, *, batch, seq, ts=256):
    T = proj.shape[0]
    W = proj.shape[1] // 2
    blocks = gate_w.shape[0]
    ns = seq // ts
    tok = lambda bi, s: bi * ns + s
    kern = functools.partial(_rglru_kernel, blocks=blocks, conv_w=conv_w.shape[0])
    return pl.pallas_call(
        kern,
        grid=(batch, ns),
        in_specs=[pl.BlockSpec((ts, W), lambda bi, s: (tok(bi, s), 0)),
                  pl.BlockSpec((ts, W), lambda bi, s: (tok(bi, s), 1)),
                  pl.BlockSpec(conv_w.shape, lambda bi, s: (0, 0)),
                  pl.BlockSpec((1, W), lambda bi, s: (0, 0)),
                  pl.BlockSpec(gate_w.shape, lambda bi, s: (0, 0, 0)),
                  pl.BlockSpec(gate_b.shape, lambda bi, s: (0, 0, 0)),
                  pl.BlockSpec((1, W), lambda bi, s: (0, 0))],
        out_specs=pl.BlockSpec((ts, W), lambda bi, s: (tok(bi, s), 0)),
        out_shape=jax.ShapeDtypeStruct((T, W), BF16),
        scratch_shapes=[pltpu.VMEM((ts + SUBLANES, W), F32),
                        pltpu.VMEM((ts, W), F32),
                        pltpu.VMEM((ts, W), F32),
                        pltpu.VMEM((1, W), F32)],
        compiler_params=_cparams(("arbitrary", "arbitrary")),
        name="rglru",
    )(proj, proj, conv_w, conv_b, gate_w, gate_b, a_param)


def _pad_lanes(w):
    return jnp.pad(w, ((0, 0), (0, LANES - w.shape[1])))


def kernel(x, norm_mix, norm_ffn, norm_final, m_w_in, m_b_if, m_head_norm, m_w_out,
           r_w_in, r_conv_w, r_conv_b, r_gate_w, r_gate_b, r_a_param, r_w_out,
           ffn_w_in, ffn_w_out):
    B, S, D = x.shape
    T = B * S
    H = M_HEADS
    v_dim = m_w_out.shape[1]
    qk_dim = (m_w_in.shape[2] - 2 * v_dim - 2 * H) // 2
    n_main = 2 * qk_dim + 2 * v_dim
    row = lambda v: v.reshape(1, -1)

    h = x.reshape(T, D)

    w_in = m_w_in[0]
    w_gates = jnp.concatenate([_pad_lanes(w_in[:, n_main:n_main + H]),
                               _pad_lanes(w_in[:, n_main + H:])], axis=1).astype(BF16)
    proj, gates = _norm_proj(h, row(norm_mix[0]), w_in[:, :n_main].astype(BF16), w_gates)
    b_i = _pad_lanes(row(m_b_if[0, :H]))
    b_f = _pad_lanes(row(m_b_if[0, H:]))
    mixed = _mlstm(proj, gates, b_i, b_f, row(m_head_norm[0]), batch=B, seq=S, heads=H,
                   dk=qk_dim // H, dv=v_dim // H)
    h = _proj_residual(mixed, m_w_out[0].astype(BF16), h)
    h = _ffn(h, row(norm_ffn[0]), ffn_w_in[0].astype(BF16), ffn_w_out[0].astype(BF16),
             row(norm_final), final_norm=False)

    proj = _norm_proj(h, row(norm_mix[1]), r_w_in[0].astype(BF16))
    gate_b = r_gate_b[0].reshape(R_BLOCKS, 1, -1)
    mixed = _rglru(proj, r_conv_w[0], row(r_conv_b[0]), r_gate_w[0].astype(BF16), gate_b,
                   row(r_a_param[0]), batch=B, seq=S)
    h = _proj_residual(mixed, r_w_out[0].astype(BF16), h)
    h = _ffn(h, row(norm_ffn[1]), ffn_w_in[1].astype(BF16), ffn_w_out[1].astype(BF16),
             row(norm_final), final_norm=True)
    return h.reshape(B, S, D)
```

```python
import functools
import math

import jax
import jax.numpy as jnp
from jax import lax
from jax.experimental import pallas as pl
from jax.experimental.pallas import tpu as pltpu

F32 = jnp.float32
BF16 = jnp.bfloat16

EPS = 1e-6
M_HEADS = 8
R_BLOCKS = 8
R_CONV_W = 4
R_C = 8.0
LANES = 128
SUBLANES = 8
VMEM_LIMIT = 56 * 1024 * 1024

MLSTM_CHUNK = 256
ROW_CHUNK = 256


def _cparams(sem):
    return pltpu.CompilerParams(dimension_semantics=sem, vmem_limit_bytes=VMEM_LIMIT)


def _rms_rows(x, g):
    ms = jnp.mean(x * x, axis=-1, keepdims=True)
    return x * lax.rsqrt(ms + EPS) * g


def _norm_proj_kernel(x_ref, g_ref, w_ref, o_ref, xn_ref):
    @pl.when(pl.program_id(1) == 0)
    def _():
        def body(r, c):
            rows = pl.ds(pl.multiple_of(r * ROW_CHUNK, ROW_CHUNK), ROW_CHUNK)
            xn_ref[rows, :] = _rms_rows(x_ref[rows, :], g_ref[...]).astype(BF16)
            return c
        lax.fori_loop(0, x_ref.shape[0] // ROW_CHUNK, body, 0)

    o_ref[...] = jnp.dot(xn_ref[...], w_ref[...], preferred_element_type=F32).astype(o_ref.dtype)


def _norm_proj_gates_kernel(x_ref, g_ref, w_ref, wg_ref, o_ref, og_ref, xn_ref):
    @pl.when(pl.program_id(1) == 0)
    def _():
        def body(r, c):
            rows = pl.ds(pl.multiple_of(r * ROW_CHUNK, ROW_CHUNK), ROW_CHUNK)
            xn_ref[rows, :] = _rms_rows(x_ref[rows, :], g_ref[...]).astype(BF16)
            return c
        lax.fori_loop(0, x_ref.shape[0] // ROW_CHUNK, body, 0)
        og_ref[...] = jnp.dot(xn_ref[...], wg_ref[...], preferred_element_type=F32)

    o_ref[...] = jnp.dot(xn_ref[...], w_ref[...], preferred_element_type=F32).astype(o_ref.dtype)


def _norm_proj(x, gain, w, layer, n_cols, w_gates=None, *, tm=1024, tn=2048):
    T, D = x.shape
    grid = (T // tm, n_cols // tn)
    x_spec = pl.BlockSpec((tm, D), lambda i, j: (i, 0))
    g_spec = pl.BlockSpec((1, D), lambda i, j: (0, 0))
    w_spec = pl.BlockSpec((None, D, tn), lambda i, j: (layer, 0, j))
    o_spec = pl.BlockSpec((tm, tn), lambda i, j: (i, j))
    scratch = [pltpu.VMEM((tm, D), BF16)]
    if w_gates is None:
        return pl.pallas_call(
            _norm_proj_kernel,
            grid=grid,
            in_specs=[x_spec, g_spec, w_spec],
            out_specs=o_spec,
            out_shape=jax.ShapeDtypeStruct((T, n_cols), BF16),
            scratch_shapes=scratch,
            compiler_params=_cparams(("arbitrary", "arbitrary")),
            name="norm_proj",
        )(x, gain, w)
    NG = w_gates.shape[1]
    return pl.pallas_call(
        _norm_proj_gates_kernel,
        grid=grid,
        in_specs=[x_spec, g_spec, w_spec,
                  pl.BlockSpec((D, NG), lambda i, j: (0, 0), pipeline_mode=pl.Buffered(1))],
        out_specs=[o_spec, pl.BlockSpec((tm, NG), lambda i, j: (i, 0))],
        out_shape=[jax.ShapeDtypeStruct((T, n_cols), BF16), jax.ShapeDtypeStruct((T, NG), F32)],
        scratch_shapes=scratch,
        compiler_params=_cparams(("arbitrary", "arbitrary")),
        name="norm_proj_gates",
    )(x, gain, w, w_gates)


def _prefix_rows(x, op, fill):
    L = x.shape[0]
    row = lax.broadcasted_iota(jnp.int32, x.shape, 0)
    d = 1
    while d < L:
        shifted = pltpu.roll(x, d, axis=0)
        x = op(x, jnp.where(row >= d, shifted, fill))
        d *= 2
    return x


def _mlstm_kernel(q_ref, k_ref, v_ref, o_ref, gi_ref, gf_ref, bi_ref, bf_ref, hn_ref,
                  out_ref, c_ref, n_ref, m_ref, *, heads, dk, dv):
    L = q_ref.shape[0]
    scale = dk ** -0.5

    @pl.when(pl.program_id(1) == 0)
    def _():
        c_ref[...] = jnp.zeros_like(c_ref)
        n_ref[...] = jnp.zeros_like(n_ref)
        m_ref[...] = jnp.zeros_like(m_ref)

    gi = gi_ref[...] + bi_ref[...]
    lf = jax.nn.log_sigmoid(gf_ref[...] + bf_ref[...])
    b = _prefix_rows(lf, jnp.add, 0.0)
    m_prev = m_ref[...]
    src = gi - b
    m_inter = b + m_prev
    m_t = jnp.maximum(m_inter, b + _prefix_rows(src, jnp.maximum, -jnp.inf))
    scale_inter = jnp.exp(m_inter - m_t)
    tgt = b - m_t + math.log(scale)
    floor = jnp.exp(-m_t)
    b_last = b[L - 1:L, :]
    g = b_last - b + gi
    m_new = jnp.maximum(b_last + m_prev, jnp.max(g, axis=0, keepdims=True))
    wk = jnp.exp(g - m_new) * scale
    decay = jnp.exp(b_last + m_prev - m_new)
    m_ref[...] = m_new
    src_t = src.T

    row = lax.broadcasted_iota(jnp.int32, (L, L), 0)
    col = lax.broadcasted_iota(jnp.int32, (L, L), 1)
    causal = col <= row

    for h in range(heads):
        ksl = slice(h * dk, (h + 1) * dk)
        vsl = slice(h * dv, (h + 1) * dv)
        qb = q_ref[:, ksl]
        kb = k_ref[:, ksl]
        vb = v_ref[:, vsl]
        q = qb.astype(F32)
        qk = lax.dot_general(qb, kb, (((1,), (1,)), ((), ())),
                             preferred_element_type=F32)
        logw = tgt[:, h:h + 1] + src_t[h:h + 1, :]
        s = qk * jnp.exp(jnp.where(causal, logw, -jnp.inf))
        si = scale_inter[:, h:h + 1]
        num = (jnp.dot(s.astype(BF16), vb, preferred_element_type=F32)
               + si * jnp.dot(qb, c_ref[h].astype(BF16), preferred_element_type=F32))
        den = (jnp.sum(s, axis=-1, keepdims=True)
               + si * jnp.sum(q * n_ref[h], axis=-1, keepdims=True))
        hh = num / jnp.maximum(jnp.abs(den), floor[:, h:h + 1])
        ms = jnp.mean(hh * hh, axis=-1, keepdims=True)
        y = hh * lax.rsqrt(ms + EPS) * hn_ref[:, vsl] * jax.nn.sigmoid(o_ref[:, vsl].astype(F32))
        out_ref[:, vsl] = y.astype(out_ref.dtype)

        kw = kb.astype(F32) * wk[:, h:h + 1]
        dc = lax.dot_general(kw.astype(BF16), vb, (((0,), (0,)), ((), ())),
                             preferred_element_type=F32)
        dec = decay[:, h:h + 1]
        c_ref[h] = dec * c_ref[h] + dc
        n_ref[h] = dec * n_ref[h] + jnp.sum(kw, axis=0, keepdims=True)


def _mlstm(proj, gates, b_i, b_f, head_norm, *, batch, seq, heads, dk, dv):
    T = proj.shape[0]
    L = MLSTM_CHUNK
    nc = seq // L
    qk_w, v_w = heads * dk, heads * dv
    v_blk = (2 * qk_w) // v_w
    tok = lambda bi, c: bi * nc + c
    kern = functools.partial(_mlstm_kernel, heads=heads, dk=dk, dv=dv)
    return pl.pallas_call(
        kern,
        grid=(batch, nc),
        in_specs=[
            pl.BlockSpec((L, qk_w), lambda bi, c: (tok(bi, c), 0)),
            pl.BlockSpec((L, qk_w), lambda bi, c: (tok(bi, c), 1)),
            pl.BlockSpec((L, v_w), lambda bi, c: (tok(bi, c), v_blk)),
            pl.BlockSpec((L, v_w), lambda bi, c: (tok(bi, c), v_blk + 1)),
            pl.BlockSpec((L, LANES), lambda bi, c: (tok(bi, c), 0)),
            pl.BlockSpec((L, LANES), lambda bi, c: (tok(bi, c), 1)),
            pl.BlockSpec((1, LANES), lambda bi, c: (0, 0)),
            pl.BlockSpec((1, LANES), lambda bi, c: (0, 0)),
            pl.BlockSpec((1, v_w), lambda bi, c: (0, 0)),
        ],
        out_specs=pl.BlockSpec((L, v_w), lambda bi, c: (tok(bi, c), 0)),
        out_shape=jax.ShapeDtypeStruct((T, v_w), BF16),
        scratch_shapes=[pltpu.VMEM((heads, dk, dv), F32),
                        pltpu.VMEM((heads, 1, dk), F32),
                        pltpu.VMEM((1, LANES), F32)],
        compiler_params=_cparams(("arbitrary", "arbitrary")),
        name="mlstm",
    )(proj, proj, proj, proj, gates, gates, b_i, b_f, head_norm)


def _proj_residual_kernel(a_ref, w_ref, r_ref, o_ref):
    o_ref[...] = r_ref[...] + jnp.dot(a_ref[...], w_ref[...], preferred_element_type=F32)


def _proj_residual(a, w, layer, resid, *, tm=1024):
    T, K = a.shape
    N = w.shape[2]
    return pl.pallas_call(
        _proj_residual_kernel,
        grid=(T // tm,),
        in_specs=[pl.BlockSpec((tm, K), lambda i: (i, 0)),
                  pl.BlockSpec((None, K, N), lambda i: (layer, 0, 0), pipeline_mode=pl.Buffered(1)),
                  pl.BlockSpec((tm, N), lambda i: (i, 0))],
        out_specs=pl.BlockSpec((tm, N), lambda i: (i, 0)),
        out_shape=jax.ShapeDtypeStruct((T, N), F32),
        compiler_params=_cparams(("arbitrary",)),
        name="proj_residual",
    )(a, w, resid)


def _ffn_kernel(x_ref, g_ref, wg_ref, wu_ref, wo_ref, gf_ref, o_ref, xn_ref, *, final_norm):
    j = pl.program_id(1)
    n_chunks = x_ref.shape[0] // ROW_CHUNK

    @pl.when(j == 0)
    def _():
        def body(r, c):
            rows = pl.ds(pl.multiple_of(r * ROW_CHUNK, ROW_CHUNK), ROW_CHUNK)
            x = x_ref[rows, :]
            xn_ref[rows, :] = _rms_rows(x, g_ref[...]).astype(BF16)
            o_ref[rows, :] = x
            return c
        lax.fori_loop(0, n_chunks, body, 0)

    xn = xn_ref[...]
    gate = jnp.dot(xn, wg_ref[...], preferred_element_type=F32)
    up = jnp.dot(xn, wu_ref[...], preferred_element_type=F32)
    act = (gate * jax.nn.sigmoid(gate) * up).astype(BF16)
    o_ref[...] += jnp.dot(act, wo_ref[...], preferred_element_type=F32)

    if final_norm:
        @pl.when(j == pl.num_programs(1) - 1)
        def _():
            def body(r, c):
                rows = pl.ds(pl.multiple_of(r * ROW_CHUNK, ROW_CHUNK), ROW_CHUNK)
                o_ref[rows, :] = _rms_rows(o_ref[rows, :], gf_ref[...])
                return c
            lax.fori_loop(0, n_chunks, body, 0)


def _ffn(x, gain, w_in, w_out, layer, gain_final, *, final_norm, tm=1024, tf=512):
    T, D = x.shape
    d_ff = w_out.shape[1]
    nj = d_ff // tf
    kern = functools.partial(_ffn_kernel, final_norm=final_norm)
    return pl.pallas_call(
        kern,
        grid=(T // tm, nj),
        in_specs=[pl.BlockSpec((tm, D), lambda i, j: (i, 0)),
                  pl.BlockSpec((1, D), lambda i, j: (0, 0)),
                  pl.BlockSpec((None, D, tf), lambda i, j: (layer, 0, j)),
                  pl.BlockSpec((None, D, tf), lambda i, j: (layer, 0, nj + j)),
                  pl.BlockSpec((None, tf, D), lambda i, j: (layer, j, 0)),
                  pl.BlockSpec((1, D), lambda i, j: (0, 0))],
        out_specs=pl.BlockSpec((tm, D), lambda i, j: (i, 0)),
        out_shape=jax.ShapeDtypeStruct((T, D), F32),
        scratch_shapes=[pltpu.VMEM((tm, D), BF16)],
        compiler_params=_cparams(("arbitrary", "arbitrary")),
        name="ffn_final" if final_norm else "ffn",
    )(x, gain, w_in, w_in, w_out, gain_final)


def _rglru_kernel(gate_ref, rec_ref, cw_ref, cb_ref, gw_ref, gb_ref, ap_ref, y_ref,
                  buf_ref, a_ref, u_ref, h_ref, *, blocks, conv_w):
    ts, W = rec_ref.shape
    bw = W // blocks
    pad = SUBLANES

    @pl.when(pl.program_id(1) == 0)
    def _():
        buf_ref[0:pad, :] = jnp.zeros((pad, W), F32)
        h_ref[...] = jnp.zeros_like(h_ref)

    buf_ref[pad:pad + ts, :] = rec_ref[...].astype(F32)

    for blk in range(blocks):
        cs = slice(blk * bw, (blk + 1) * bw)
        conv = cb_ref[:, cs] + cw_ref[conv_w - 1:conv_w, cs] * buf_ref[pad:pad + ts, cs]
        for tap in range(conv_w - 1):
            off = pad - (conv_w - 1) + tap
            conv = conv + cw_ref[tap:tap + 1, cs] * buf_ref[off:off + ts, cs]
        pre = jnp.dot(conv.astype(BF16), gw_ref[blk], preferred_element_type=F32) + gb_ref[blk]
        r = jax.nn.sigmoid(pre[:, :bw])
        i = jax.nn.sigmoid(pre[:, bw:])
        log_a = (R_C * jax.nn.log_sigmoid(ap_ref[:, cs])) * r
        a = jnp.exp(log_a)
        a_ref[:, cs] = a
        u_ref[:, cs] = jnp.sqrt(jnp.tanh(-log_a) * (a * a + 1.0)) * (i * conv)

    buf_ref[0:pad, :] = buf_ref[ts:ts + pad, :]

    row = lax.broadcasted_iota(jnp.int32, (SUBLANES, W), 0)

    def scan_rows(r, h_prev):
        rows = pl.ds(pl.multiple_of(r * SUBLANES, SUBLANES), SUBLANES)
        a = a_ref[rows, :]
        u = u_ref[rows, :]
        d = 1
        while d < SUBLANES:
            keep = row >= d
            u = jnp.where(keep, a * pltpu.roll(u, d, axis=0) + u, u)
            a = jnp.where(keep, a * pltpu.roll(a, d, axis=0), a)
            d *= 2
        h = a * h_prev + u
        u_ref[rows, :] = h
        return h[SUBLANES - 1:SUBLANES, :]

    h_ref[...] = lax.fori_loop(0, ts // SUBLANES, scan_rows, h_ref[...])

    for blk in range(blocks):
        cs = slice(blk * bw, (blk + 1) * bw)
        y_ref[:, cs] = (jax.nn.gelu(gate_ref[:, cs].astype(F32)) * u_ref[:, cs]).astype(y_ref.dtype)


def _rglru(proj, conv_w, conv_b, gate_w, gate_b, a_param, *, batch, seq, ts=256):
    T = proj.shape[0]
    W = proj.shape[1] // 2
    blocks = gate_w.shape[1]
    ns = seq // ts
    tok = lambda bi, s: bi * ns + s
    kern = functools.partial(_rglru_kernel, blocks=blocks, conv_w=conv_w.shape[0])
    return pl.pallas_call(
        kern,
        grid=(batch, ns),
        in_specs=[pl.BlockSpec((ts, W), lambda bi, s: (tok(bi, s), 0)),
                  pl.BlockSpec((ts, W), lambda bi, s: (tok(bi, s), 1)),
                  pl.BlockSpec(conv_w.shape, lambda bi, s: (0, 0)),
                  pl.BlockSpec((1, W), lambda bi, s: (0, 0)),
                  pl.BlockSpec((None,) + gate_w.shape[1:], lambda bi, s: (0, 0, 0, 0)),
                  pl.BlockSpec(gate_b.shape, lambda bi, s: (0, 0, 0)),
                  pl.BlockSpec((1, W), lambda bi, s: (0, 0))],
        out_specs=pl.BlockSpec((ts, W), lambda bi, s: (tok(bi, s), 0)),
        out_shape=jax.ShapeDtypeStruct((T, W), BF16),
        scratch_shapes=[pltpu.VMEM((ts + SUBLANES, W), F32),
                        pltpu.VMEM((ts, W), F32),
                        pltpu.VMEM((ts, W), F32),
                        pltpu.VMEM((1, W), F32)],
        compiler_params=_cparams(("arbitrary", "arbitrary")),
        name="rglru",
    )(proj, proj, conv_w, conv_b, gate_w, gate_b, a_param)


def _pad_lanes(w):
    return jnp.pad(w, ((0, 0), (0, LANES - w.shape[1])))


def kernel(x, norm_mix, norm_ffn, norm_final, m_w_in, m_b_if, m_head_norm, m_w_out,
           r_w_in, r_conv_w, r_conv_b, r_gate_w, r_gate_b, r_a_param, r_w_out,
           ffn_w_in, ffn_w_out):
    B, S, D = x.shape
    T = B * S
    H = M_HEADS
    v_dim = m_w_out.shape[1]
    qk_dim = (m_w_in.shape[2] - 2 * v_dim - 2 * H) // 2
    n_main = 2 * qk_dim + 2 * v_dim
    row = lambda v: v.reshape(1, -1)

    h = x.reshape(T, D)

    ffn_w_in_b = ffn_w_in.astype(BF16)
    ffn_w_out_b = ffn_w_out.astype(BF16)

    w_if = m_w_in[0, :, n_main:]
    w_gates = jnp.concatenate([_pad_lanes(w_if[:, :H]), _pad_lanes(w_if[:, H:])], axis=1).astype(BF16)
    proj, gates = _norm_proj(h, row(norm_mix[0]), m_w_in.astype(BF16), 0, n_main, w_gates)
    b_i = _pad_lanes(row(m_b_if[0, :H]))
    b_f = _pad_lanes(row(m_b_if[0, H:]))
    mixed = _mlstm(proj, gates, b_i, b_f, row(m_head_norm[0]), batch=B, seq=S, heads=H,
                   dk=qk_dim // H, dv=v_dim // H)
    h = _proj_residual(mixed, m_w_out.astype(BF16), 0, h)
    h = _ffn(h, row(norm_ffn[0]), ffn_w_in_b, ffn_w_out_b, 0, row(norm_final), final_norm=False)

    proj = _norm_proj(h, row(norm_mix[1]), r_w_in.astype(BF16), 0, r_w_in.shape[2])
    gate_b = r_gate_b[0].reshape(R_BLOCKS, 1, -1)
    mixed = _rglru(proj, r_conv_w[0], row(r_conv_b[0]), r_gate_w.astype(BF16), gate_b,
                   row(r_a_param[0]), batch=B, seq=S)
    h = _proj_residual(mixed, r_w_out.astype(BF16), 0, h)
    h = _ffn(h, row(norm_ffn[1]), ffn_w_in_b, ffn_w_out_b, 1, row(norm_final), final_norm=True)
    return h.reshape(B, S, D)
```

```python
import functools
import math

import jax
import jax.numpy as jnp
from jax import lax
from jax.experimental import pallas as pl
from jax.experimental.pallas import tpu as pltpu

F32 = jnp.float32
BF16 = jnp.bfloat16

EPS = 1e-6
M_HEADS = 8
R_BLOCKS = 8
R_C = 8.0
LANES = 128
SUBLANES = 8
VMEM_LIMIT = 60 * 1024 * 1024

MLSTM_CHUNK = 256
RGLRU_TILE = 256
ROW_CHUNK = 256


def _cparams(sem):
    return pltpu.CompilerParams(dimension_semantics=sem, vmem_limit_bytes=VMEM_LIMIT)


def _rms_rows(x, g):
    ms = jnp.mean(x * x, axis=-1, keepdims=True)
    return x * lax.rsqrt(ms + EPS) * g


def _slot_major(w, slots):
    K, N = w.shape
    return w.reshape(K, slots, N // slots).transpose(1, 0, 2)


def _prefix_rows(x, op, fill):
    L = x.shape[0]
    row = lax.broadcasted_iota(jnp.int32, x.shape, 0)
    d = 1
    while d < L:
        shifted = pltpu.roll(x, d, axis=0)
        x = op(x, jnp.where(row >= d, shifted, fill))
        d *= 2
    return x


def _mlstm_layer_kernel(x_ref, g_ref, win_ref, wg_ref, bi_ref, bf_ref, hn_ref, out_ref,
                        p_ref, gates_ref, c_ref, n_ref, m_ref, *, heads, dk, dv, nc):
    L, D = x_ref.shape
    scale = dk ** -0.5
    t = pl.program_id(0)
    cur = t % 2
    pa_ref = p_ref.at[cur]
    pb_ref = p_ref.at[1 - cur]
    qk_w, v_w = heads * dk, heads * dv
    na = win_ref.shape[2]

    @pl.when(t == 0)
    def _():
        p_ref[...] = jnp.zeros_like(p_ref)
        gates_ref[...] = jnp.zeros_like(gates_ref)

    @pl.when((t == 0) | ((t - 1) % nc == 0))
    def _():
        c_ref[...] = jnp.zeros_like(c_ref)
        n_ref[...] = jnp.zeros_like(n_ref)
        m_ref[...] = jnp.zeros_like(m_ref)

    xn = _rms_rows(x_ref[...], g_ref[...]).astype(BF16)
    gates_new = jnp.dot(xn, wg_ref[...], preferred_element_type=F32)

    gates = gates_ref[...]
    gi = gates[:, :LANES] + bi_ref[...]
    lf = jax.nn.log_sigmoid(gates[:, LANES:] + bf_ref[...])
    b = _prefix_rows(lf, jnp.add, 0.0)
    m_prev = m_ref[...]
    src = gi - b
    m_inter = b + m_prev
    m_t = jnp.maximum(m_inter, b + _prefix_rows(src, jnp.maximum, -jnp.inf))
    scale_inter = jnp.exp(m_inter - m_t)
    tgt = b - m_t + math.log(scale)
    floor = jnp.exp(-m_t)
    b_last = b[L - 1:L, :]
    g = b_last - b + gi
    m_new = jnp.maximum(b_last + m_prev, jnp.max(g, axis=0, keepdims=True))
    wk = jnp.exp(g - m_new) * scale
    decay = jnp.exp(b_last + m_prev - m_new)
    m_ref[...] = m_new
    src_t = src.T

    row = lax.broadcasted_iota(jnp.int32, (L, L), 0)
    col = lax.broadcasted_iota(jnp.int32, (L, L), 1)
    causal = col <= row

    for h in range(heads):
        qsl = slice(h * dk, (h + 1) * dk)
        ksl = slice(qk_w + h * dk, qk_w + (h + 1) * dk)
        vsl = slice(2 * qk_w + h * dv, 2 * qk_w + (h + 1) * dv)
        osl = slice(2 * qk_w + v_w + h * dv, 2 * qk_w + v_w + (h + 1) * dv)
        ysl = slice(h * dv, (h + 1) * dv)
        qb = pb_ref[:, qsl]
        kb = pb_ref[:, ksl]
        vb = pb_ref[:, vsl]
        q = qb.astype(F32)
        qk = lax.dot_general(qb, kb, (((1,), (1,)), ((), ())),
                             preferred_element_type=F32)
        logw = tgt[:, h:h + 1] + src_t[h:h + 1, :]
        s = qk * jnp.exp(jnp.where(causal, logw, -jnp.inf))
        si = scale_inter[:, h:h + 1]
        num = (jnp.dot(s.astype(BF16), vb, preferred_element_type=F32)
               + si * jnp.dot(qb, c_ref[h].astype(BF16), preferred_element_type=F32))
        den = (jnp.sum(s, axis=-1, keepdims=True)
               + si * jnp.sum(q * n_ref[h], axis=-1, keepdims=True))
        inv = 1.0 / jnp.maximum(jnp.abs(den), floor[:, h:h + 1])
        msn = jnp.mean(num * num, axis=-1, keepdims=True)
        rs = inv * lax.rsqrt(inv * inv * msn + EPS)
        y = num * rs * hn_ref[:, ysl] * jax.nn.sigmoid(pb_ref[:, osl].astype(F32))
        out_ref[:, ysl] = y.astype(out_ref.dtype)

        kw = kb.astype(F32) * wk[:, h:h + 1]
        dc = lax.dot_general(kw.astype(BF16), vb, (((0,), (0,)), ((), ())),
                             preferred_element_type=F32)
        dec = decay[:, h:h + 1]
        c_ref[h] = dec * c_ref[h] + dc
        n_ref[h] = dec * n_ref[h] + jnp.sum(kw, axis=0, keepdims=True)

        ca = slice(h * na, (h + 1) * na)
        pa_ref[:, ca] = jnp.dot(xn, win_ref[h], preferred_element_type=F32).astype(BF16)

    gates_ref[...] = gates_new


def _mlstm_layer(x, gain, w_in_s, w_gates, b_i, b_f, head_norm, *, batch, seq, heads, dk, dv):
    T, D = x.shape
    L = MLSTM_CHUNK
    nc = seq // L
    n = batch * nc
    v_w = heads * dv
    n_main = 2 * heads * dk + 2 * v_w
    kern = functools.partial(_mlstm_layer_kernel, heads=heads, dk=dk, dv=dv, nc=nc)
    const2 = lambda t: (0, 0)
    one = pl.Buffered(1)
    return pl.pallas_call(
        kern,
        grid=(n + 1,),
        in_specs=[pl.BlockSpec((L, D), lambda t: (jnp.minimum(t, n - 1), 0)),
                  pl.BlockSpec((1, D), const2),
                  pl.BlockSpec(w_in_s.shape, lambda t: (0, 0, 0), pipeline_mode=one),
                  pl.BlockSpec(w_gates.shape, const2, pipeline_mode=one),
                  pl.BlockSpec((1, LANES), const2),
                  pl.BlockSpec((1, LANES), const2),
                  pl.BlockSpec((1, v_w), const2)],
        out_specs=pl.BlockSpec((L, v_w), lambda t: (jnp.maximum(t - 1, 0), 0)),
        out_shape=jax.ShapeDtypeStruct((T, v_w), BF16),
        scratch_shapes=[pltpu.VMEM((2, L, n_main), BF16),
                        pltpu.VMEM((L, 2 * LANES), F32),
                        pltpu.VMEM((heads, dk, dv), F32),
                        pltpu.VMEM((heads, 1, dk), F32),
                        pltpu.VMEM((1, LANES), F32)],
        compiler_params=_cparams(("arbitrary",)),
        name="mlstm_layer",
    )(x, gain, w_in_s, w_gates, b_i, b_f, head_norm)


def _proj_residual_kernel(a_ref, w_ref, r_ref, o_ref):
    o_ref[...] = r_ref[...] + jnp.dot(a_ref[...], w_ref[...], preferred_element_type=F32)


def _proj_residual(a, w, layer, resid, *, tm=1024):
    T, K = a.shape
    N = w.shape[2]
    return pl.pallas_call(
        _proj_residual_kernel,
        grid=(T // tm,),
        in_specs=[pl.BlockSpec((tm, K), lambda i: (i, 0)),
                  pl.BlockSpec((None, K, N), lambda i: (layer, 0, 0), pipeline_mode=pl.Buffered(1)),
                  pl.BlockSpec((tm, N), lambda i: (i, 0))],
        out_specs=pl.BlockSpec((tm, N), lambda i: (i, 0)),
        out_shape=jax.ShapeDtypeStruct((T, N), F32),
        compiler_params=_cparams(("arbitrary",)),
        name="proj_residual",
    )(a, w, resid)


def _ffn_kernel(x_ref, g_ref, wg_ref, wu_ref, wo_ref, gf_ref, o_ref, xn_ref, *, final_norm):
    j = pl.program_id(1)
    n_chunks = x_ref.shape[0] // ROW_CHUNK

    @pl.when(j == 0)
    def _():
        def body(r, c):
            rows = pl.ds(pl.multiple_of(r * ROW_CHUNK, ROW_CHUNK), ROW_CHUNK)
            x = x_ref[rows, :]
            xn_ref[rows, :] = _rms_rows(x, g_ref[...]).astype(BF16)
            o_ref[rows, :] = x
            return c
        lax.fori_loop(0, n_chunks, body, 0)

    xn = xn_ref[...]
    gate = jnp.dot(xn, wg_ref[...], preferred_element_type=F32)
    up = jnp.dot(xn, wu_ref[...], preferred_element_type=F32)
    act = (gate * jax.nn.sigmoid(gate) * up).astype(BF16)
    o_ref[...] += jnp.dot(act, wo_ref[...], preferred_element_type=F32)

    if final_norm:
        @pl.when(j == pl.num_programs(1) - 1)
        def _():
            def body(r, c):
                rows = pl.ds(pl.multiple_of(r * ROW_CHUNK, ROW_CHUNK), ROW_CHUNK)
                o_ref[rows, :] = _rms_rows(o_ref[rows, :], gf_ref[...])
                return c
            lax.fori_loop(0, n_chunks, body, 0)


def _ffn(x, gain, w_in, w_out, layer, gain_final, *, final_norm, tm=1024, tf=512):
    T, D = x.shape
    d_ff = w_out.shape[1]
    nj = d_ff // tf
    kern = functools.partial(_ffn_kernel, final_norm=final_norm)
    return pl.pallas_call(
        kern,
        grid=(T // tm, nj),
        in_specs=[pl.BlockSpec((tm, D), lambda i, j: (i, 0)),
                  pl.BlockSpec((1, D), lambda i, j: (0, 0)),
                  pl.BlockSpec((None, D, tf), lambda i, j: (layer, 0, j)),
                  pl.BlockSpec((None, D, tf), lambda i, j: (layer, 0, nj + j)),
                  pl.BlockSpec((None, tf, D), lambda i, j: (layer, j, 0)),
                  pl.BlockSpec((1, D), lambda i, j: (0, 0))],
        out_specs=pl.BlockSpec((tm, D), lambda i, j: (i, 0)),
        out_shape=jax.ShapeDtypeStruct((T, D), F32),
        scratch_shapes=[pltpu.VMEM((tm, D), BF16)],
        compiler_params=_cparams(("arbitrary", "arbitrary")),
        name="ffn_final" if final_norm else "ffn",
    )(x, gain, w_in, w_in, w_out, gain_final)


def _rglru_layer_kernel(xa_ref, xc_ref, g_ref, win_ref, cw_ref, cb_ref, gw_ref, gb_ref, ap_ref, wout_ref,
                        o_ref, p_ref, y_ref, buf_ref, a_ref, u_ref, h_ref, *, blocks, conv_w, ns):
    ts, W = xa_ref.shape
    bw = W // blocks
    pad = SUBLANES
    t = pl.program_id(0)
    cur = t % 2
    pa_ref = p_ref.at[cur]
    pb_ref = p_ref.at[1 - cur]
    ya_ref = y_ref.at[1 - cur]
    yb_ref = y_ref.at[cur]

    @pl.when(t == 0)
    def _():
        p_ref[...] = jnp.zeros_like(p_ref)
        y_ref[...] = jnp.zeros_like(y_ref)

    @pl.when((t == 0) | ((t - 1) % ns == 0))
    def _():
        buf_ref[0:pad, :] = jnp.zeros((pad, W), F32)
        h_ref[...] = jnp.zeros_like(h_ref)

    xn = _rms_rows(xa_ref[...], g_ref[...]).astype(BF16)
    na = 2 * W // blocks
    nc = W // blocks

    buf_ref[pad:pad + ts, :] = pb_ref[:, W:].astype(F32)
    for blk in range(blocks):
        cs = slice(blk * bw, (blk + 1) * bw)
        conv = cb_ref[:, cs] + cw_ref[conv_w - 1:conv_w, cs] * buf_ref[pad:pad + ts, cs]
        for tap in range(conv_w - 1):
            off = pad - (conv_w - 1) + tap
            conv = conv + cw_ref[tap:tap + 1, cs] * buf_ref[off:off + ts, cs]
        pre = jnp.dot(conv.astype(BF16), gw_ref[blk], preferred_element_type=F32) + gb_ref[blk]
        r = jax.nn.sigmoid(pre[:, :bw])
        i = jax.nn.sigmoid(pre[:, bw:])
        log_a = (R_C * jax.nn.log_sigmoid(ap_ref[:, cs])) * r
        a = jnp.exp(log_a)
        a_ref[:, cs] = a
        z = jnp.tanh(-log_a) * (a * a + 1.0)
        mult = jnp.where(z == 0.0, 0.0, z * lax.rsqrt(z))
        u_ref[:, cs] = mult * (i * conv)
        ca = slice(blk * na, (blk + 1) * na)
        pa_ref[:, ca] = jnp.dot(xn, win_ref[blk], preferred_element_type=F32).astype(BF16)
    buf_ref[0:pad, :] = buf_ref[ts:ts + pad, :]

    row = lax.broadcasted_iota(jnp.int32, (SUBLANES, W), 0)
    yb = yb_ref[...]
    per_slot = ts // SUBLANES // blocks
    h_prev = h_ref[...]
    for blk in range(blocks):
        for r in range(blk * per_slot, (blk + 1) * per_slot):
            rows = slice(r * SUBLANES, (r + 1) * SUBLANES)
            a = a_ref[rows, :]
            u = u_ref[rows, :]
            d = 1
            while d < SUBLANES:
                keep = row >= d
                u = jnp.where(keep, a * pltpu.roll(u, d, axis=0) + u, u)
                a = jnp.where(keep, a * pltpu.roll(a, d, axis=0), a)
                d *= 2
            h = a * h_prev + u
            u_ref[rows, :] = h
            h_prev = h[SUBLANES - 1:SUBLANES, :]
        cc = slice(blk * nc, (blk + 1) * nc)
        o_ref[:, cc] = xc_ref[:, cc] + jnp.dot(yb, wout_ref[blk], preferred_element_type=F32)
    h_ref[...] = h_prev

    for blk in range(blocks):
        cs = slice(blk * bw, (blk + 1) * bw)
        ya_ref[:, cs] = (jax.nn.gelu(pb_ref[:, cs].astype(F32)) * u_ref[:, cs]).astype(BF16)


def _rglru_layer(x, gain, w_in_s, conv_w, conv_b, gate_w, gate_b, a_param, w_out_s, *, batch, seq):
    T, D = x.shape
    ts = RGLRU_TILE
    blocks, W = w_out_s.shape[0], w_out_s.shape[1]
    ns = seq // ts
    n = batch * ns
    kern = functools.partial(_rglru_layer_kernel, blocks=blocks, conv_w=conv_w.shape[0], ns=ns)
    const2 = lambda t: (0, 0)
    const3 = lambda t: (0, 0, 0)
    one = pl.Buffered(1)
    return pl.pallas_call(
        kern,
        grid=(n + 2,),
        in_specs=[pl.BlockSpec((ts, D), lambda t: (jnp.minimum(t, n - 1), 0)),
                  pl.BlockSpec((ts, D), lambda t: (jnp.clip(t - 2, 0, n - 1), 0)),
                  pl.BlockSpec((1, D), const2),
                  pl.BlockSpec(w_in_s.shape, const3, pipeline_mode=one),
                  pl.BlockSpec(conv_w.shape, const2),
                  pl.BlockSpec((1, W), const2),
                  pl.BlockSpec((None,) + gate_w.shape[1:], lambda t: (0, 0, 0, 0), pipeline_mode=one),
                  pl.BlockSpec(gate_b.shape, const3),
                  pl.BlockSpec((1, W), const2),
                  pl.BlockSpec(w_out_s.shape, const3, pipeline_mode=one)],
        out_specs=pl.BlockSpec((ts, D), lambda t: (jnp.clip(t - 2, 0, n - 1), 0)),
        out_shape=jax.ShapeDtypeStruct((T, D), F32),
        scratch_shapes=[pltpu.VMEM((2, ts, 2 * W), BF16),
                        pltpu.VMEM((2, ts, W), BF16),
                        pltpu.VMEM((ts + SUBLANES, W), F32),
                        pltpu.VMEM((ts, W), F32),
                        pltpu.VMEM((ts, W), F32),
                        pltpu.VMEM((1, W), F32)],
        compiler_params=_cparams(("arbitrary",)),
        name="rglru_layer",
    )(x, x, gain, w_in_s, conv_w, conv_b, gate_w, gate_b, a_param, w_out_s)


def _pad_lanes(w):
    return jnp.pad(w, ((0, 0), (0, LANES - w.shape[1])))


def kernel(x, norm_mix, norm_ffn, norm_final, m_w_in, m_b_if, m_head_norm, m_w_out,
           r_w_in, r_conv_w, r_conv_b, r_gate_w, r_gate_b, r_a_param, r_w_out,
           ffn_w_in, ffn_w_out):
    B, S, D = x.shape
    T = B * S
    H = M_HEADS
    v_dim = m_w_out.shape[1]
    qk_dim = (m_w_in.shape[2] - 2 * v_dim - 2 * H) // 2
    n_main = 2 * qk_dim + 2 * v_dim
    row = lambda v: v.reshape(1, -1)

    h = x.reshape(T, D)

    ffn_w_in_b = ffn_w_in.astype(BF16)
    ffn_w_out_b = ffn_w_out.astype(BF16)

    w_if = m_w_in[0, :, n_main:]
    w_gates = jnp.concatenate([_pad_lanes(w_if[:, :H]), _pad_lanes(w_if[:, H:])], axis=1).astype(BF16)
    w_in_s = _slot_major(m_w_in[0, :, :n_main].astype(BF16), H)
    b_i = _pad_lanes(row(m_b_if[0, :H]))
    b_f = _pad_lanes(row(m_b_if[0, H:]))
    mixed = _mlstm_layer(h, row(norm_mix[0]), w_in_s, w_gates, b_i, b_f, row(m_head_norm[0]),
                         batch=B, seq=S, heads=H, dk=qk_dim // H, dv=v_dim // H)
    h = _proj_residual(mixed, m_w_out.astype(BF16), 0, h)
    h = _ffn(h, row(norm_ffn[0]), ffn_w_in_b, ffn_w_out_b, 0, row(norm_final), final_norm=False)

    gate_b = r_gate_b[0].reshape(R_BLOCKS, 1, -1)
    h = _rglru_layer(h, row(norm_mix[1]), _slot_major(r_w_in[0].astype(BF16), R_BLOCKS),
                     r_conv_w[0], row(r_conv_b[0]), r_gate_w.astype(BF16), gate_b, row(r_a_param[0]),
                     _slot_major(r_w_out[0].astype(BF16), R_BLOCKS), batch=B, seq=S)
    h = _ffn(h, row(norm_ffn[1]), ffn_w_in_b, ffn_w_out_b, 1, row(norm_final), final_norm=True)
    return h.reshape(B, S, D)
```

```python
import functools
import math

import jax
import jax.numpy as jnp
from jax import lax
from jax.experimental import pallas as pl
from jax.experimental.pallas import tpu as pltpu

F32 = jnp.float32
BF16 = jnp.bfloat16

EPS = 1e-6
M_HEADS = 8
R_BLOCKS = 8
R_C = 8.0
LANES = 128
SUBLANES = 8
VMEM_LIMIT = 60 * 1024 * 1024

MLSTM_CHUNK = 256
RGLRU_TILE = 256
ROW_CHUNK = 256


def _cparams(sem):
    return pltpu.CompilerParams(dimension_semantics=sem, vmem_limit_bytes=VMEM_LIMIT)


def _rms_rows(x, g):
    ms = jnp.mean(x * x, axis=-1, keepdims=True)
    return x * lax.rsqrt(ms + EPS) * g


def _prefix_rows(x, op, fill):
    L = x.shape[0]
    row = lax.broadcasted_iota(jnp.int32, x.shape, 0)
    d = 1
    while d < L:
        shifted = pltpu.roll(x, d, axis=0)
        x = op(x, jnp.where(row >= d, shifted, fill))
        d *= 2
    return x


def _mlstm_layer_kernel(x_ref, g_ref, *refs, heads, dk, dv, nc):
    win_refs = refs[:heads]
    wg_ref, bi_ref, bf_ref, hn_ref, out_ref, p_ref, gates_ref, c_ref, n_ref, m_ref = refs[heads:]
    L, D = x_ref.shape
    scale = dk ** -0.5
    t = pl.program_id(0)
    cur = t % 2
    pa_ref = p_ref.at[cur]
    pb_ref = p_ref.at[1 - cur]
    qk_w, v_w = heads * dk, heads * dv
    na = win_refs[0].shape[1]

    @pl.when(t == 0)
    def _():
        p_ref[...] = jnp.zeros_like(p_ref)
        gates_ref[...] = jnp.zeros_like(gates_ref)

    @pl.when((t == 0) | ((t - 1) % nc == 0))
    def _():
        c_ref[...] = jnp.zeros_like(c_ref)
        n_ref[...] = jnp.zeros_like(n_ref)
        m_ref[...] = jnp.zeros_like(m_ref)

    xn = _rms_rows(x_ref[...], g_ref[...]).astype(BF16)
    gates_new = jnp.dot(xn, wg_ref[...], preferred_element_type=F32)

    gates = gates_ref[...]
    gi = gates[:, :LANES] + bi_ref[...]
    lf = jax.nn.log_sigmoid(gates[:, LANES:] + bf_ref[...])
    b = _prefix_rows(lf, jnp.add, 0.0)
    m_prev = m_ref[...]
    src = gi - b
    m_inter = b + m_prev
    m_t = jnp.maximum(m_inter, b + _prefix_rows(src, jnp.maximum, -jnp.inf))
    scale_inter = jnp.exp(m_inter - m_t)
    tgt = b - m_t + math.log(scale)
    floor = jnp.exp(-m_t)
    b_last = b[L - 1:L, :]
    g = b_last - b + gi
    m_new = jnp.maximum(b_last + m_prev, jnp.max(g, axis=0, keepdims=True))
    wk = jnp.exp(g - m_new) * scale
    decay = jnp.exp(b_last + m_prev - m_new)
    m_ref[...] = m_new
    src_t = src.T

    row = lax.broadcasted_iota(jnp.int32, (L, L), 0)
    col = lax.broadcasted_iota(jnp.int32, (L, L), 1)
    causal = col <= row

    for h in range(heads):
        qsl = slice(h * dk, (h + 1) * dk)
        ksl = slice(qk_w + h * dk, qk_w + (h + 1) * dk)
        vsl = slice(2 * qk_w + h * dv, 2 * qk_w + (h + 1) * dv)
        osl = slice(2 * qk_w + v_w + h * dv, 2 * qk_w + v_w + (h + 1) * dv)
        ysl = slice(h * dv, (h + 1) * dv)
        qb = pb_ref[:, qsl]
        kb = pb_ref[:, ksl]
        vb = pb_ref[:, vsl]
        q = qb.astype(F32)
        qk = lax.dot_general(qb, kb, (((1,), (1,)), ((), ())),
                             preferred_element_type=F32)
        logw = tgt[:, h:h + 1] + src_t[h:h + 1, :]
        s = qk * jnp.exp(jnp.where(causal, logw, -jnp.inf))
        si = scale_inter[:, h:h + 1]
        num = (jnp.dot(s.astype(BF16), vb, preferred_element_type=F32)
               + si * jnp.dot(qb, c_ref[h].astype(BF16), preferred_element_type=F32))
        den = (jnp.sum(s, axis=-1, keepdims=True)
               + si * jnp.sum(q * n_ref[h], axis=-1, keepdims=True))
        inv = 1.0 / jnp.maximum(jnp.abs(den), floor[:, h:h + 1])
        msn = jnp.mean(num * num, axis=-1, keepdims=True)
        rs = inv * lax.rsqrt(inv * inv * msn + EPS)
        y = num * rs * hn_ref[:, ysl] * jax.nn.sigmoid(pb_ref[:, osl].astype(F32))
        out_ref[:, ysl] = y.astype(out_ref.dtype)

        kw = kb.astype(F32) * wk[:, h:h + 1]
        dc = lax.dot_general(kw.astype(BF16), vb, (((0,), (0,)), ((), ())),
                             preferred_element_type=F32)
        dec = decay[:, h:h + 1]
        c_ref[h] = dec * c_ref[h] + dc
        n_ref[h] = dec * n_ref[h] + jnp.sum(kw, axis=0, keepdims=True)

        ca = slice(h * na, (h + 1) * na)
        pa_ref[:, ca] = jnp.dot(xn, win_refs[h][...], preferred_element_type=F32).astype(BF16)

    gates_ref[...] = gates_new


def _mlstm_layer(x, gain, w_in, w_gates, b_i, b_f, head_norm, *, batch, seq, heads, dk, dv):
    T, D = x.shape
    L = MLSTM_CHUNK
    nc = seq // L
    n = batch * nc
    v_w = heads * dv
    n_main = 2 * heads * dk + 2 * v_w
    kern = functools.partial(_mlstm_layer_kernel, heads=heads, dk=dk, dv=dv, nc=nc)
    const2 = lambda t: (0, 0)
    one = pl.Buffered(1)
    w_specs = [pl.BlockSpec((None, D, n_main // heads), lambda t, s=s: (0, 0, s), pipeline_mode=one)
               for s in range(heads)]
    return pl.pallas_call(
        kern,
        grid=(n + 1,),
        in_specs=[pl.BlockSpec((L, D), lambda t: (jnp.minimum(t, n - 1), 0)),
                  pl.BlockSpec((1, D), const2),
                  *w_specs,
                  pl.BlockSpec(w_gates.shape, const2, pipeline_mode=one),
                  pl.BlockSpec((1, LANES), const2),
                  pl.BlockSpec((1, LANES), const2),
                  pl.BlockSpec((1, v_w), const2)],
        out_specs=pl.BlockSpec((L, v_w), lambda t: (jnp.maximum(t - 1, 0), 0)),
        out_shape=jax.ShapeDtypeStruct((T, v_w), BF16),
        scratch_shapes=[pltpu.VMEM((2, L, n_main), BF16),
                        pltpu.VMEM((L, 2 * LANES), F32),
                        pltpu.VMEM((heads, dk, dv), F32),
                        pltpu.VMEM((heads, 1, dk), F32),
                        pltpu.VMEM((1, LANES), F32)],
        compiler_params=_cparams(("arbitrary",)),
        name="mlstm_layer",
    )(x, gain, *([w_in] * heads), w_gates, b_i, b_f, head_norm)


def _proj_residual_kernel(a_ref, w_ref, r_ref, o_ref):
    o_ref[...] = r_ref[...] + jnp.dot(a_ref[...], w_ref[...], preferred_element_type=F32)


def _proj_residual(a, w, layer, resid, *, tm=1024):
    T, K = a.shape
    N = w.shape[2]
    return pl.pallas_call(
        _proj_residual_kernel,
        grid=(T // tm,),
        in_specs=[pl.BlockSpec((tm, K), lambda i: (i, 0)),
                  pl.BlockSpec((None, K, N), lambda i: (layer, 0, 0), pipeline_mode=pl.Buffered(1)),
                  pl.BlockSpec((tm, N), lambda i: (i, 0))],
        out_specs=pl.BlockSpec((tm, N), lambda i: (i, 0)),
        out_shape=jax.ShapeDtypeStruct((T, N), F32),
        compiler_params=_cparams(("arbitrary",)),
        name="proj_residual",
    )(a, w, resid)


def _ffn_kernel(x_ref, g_ref, wg_ref, wu_ref, wo_ref, gf_ref, o_ref, xn_ref, *, final_norm):
    j = pl.program_id(1)
    n_chunks = x_ref.shape[0] // ROW_CHUNK

    @pl.when(j == 0)
    def _():
        def body(r, c):
            rows = pl.ds(pl.multiple_of(r * ROW_CHUNK, ROW_CHUNK), ROW_CHUNK)
            x = x_ref[rows, :]
            xn_ref[rows, :] = _rms_rows(x, g_ref[...]).astype(BF16)
            o_ref[rows, :] = x
            return c
        lax.fori_loop(0, n_chunks, body, 0)

    xn = xn_ref[...]
    gate = jnp.dot(xn, wg_ref[...], preferred_element_type=F32)
    up = jnp.dot(xn, wu_ref[...], preferred_element_type=F32)
    act = (gate * jax.nn.sigmoid(gate) * up).astype(BF16)
    o_ref[...] += jnp.dot(act, wo_ref[...], preferred_element_type=F32)

    if final_norm:
        @pl.when(j == pl.num_programs(1) - 1)
        def _():
            def body(r, c):
                rows = pl.ds(pl.multiple_of(r * ROW_CHUNK, ROW_CHUNK), ROW_CHUNK)
                o_ref[rows, :] = _rms_rows(o_ref[rows, :], gf_ref[...])
                return c
            lax.fori_loop(0, n_chunks, body, 0)


def _ffn(x, gain, w_in, w_out, layer, gain_final, *, final_norm, tm=1024, tf=512):
    T, D = x.shape
    d_ff = w_out.shape[1]
    nj = d_ff // tf
    kern = functools.partial(_ffn_kernel, final_norm=final_norm)
    return pl.pallas_call(
        kern,
        grid=(T // tm, nj),
        in_specs=[pl.BlockSpec((tm, D), lambda i, j: (i, 0)),
                  pl.BlockSpec((1, D), lambda i, j: (0, 0)),
                  pl.BlockSpec((None, D, tf), lambda i, j: (layer, 0, j)),
                  pl.BlockSpec((None, D, tf), lambda i, j: (layer, 0, nj + j)),
                  pl.BlockSpec((None, tf, D), lambda i, j: (layer, j, 0)),
                  pl.BlockSpec((1, D), lambda i, j: (0, 0))],
        out_specs=pl.BlockSpec((tm, D), lambda i, j: (i, 0)),
        out_shape=jax.ShapeDtypeStruct((T, D), F32),
        scratch_shapes=[pltpu.VMEM((tm, D), BF16)],
        compiler_params=_cparams(("arbitrary", "arbitrary")),
        name="ffn_final" if final_norm else "ffn",
    )(x, gain, w_in, w_in, w_out, gain_final)


def _rglru_layer_kernel(xa_ref, xc_ref, g_ref, *refs, blocks, conv_w, ns):
    win_refs = refs[:2 * blocks]
    wout_refs = refs[2 * blocks:3 * blocks]
    (cw_ref, cb_ref, gw_ref, gb_ref, ap_ref,
     o_ref, p_ref, y_ref, buf_ref, a_seg, u_seg, h_seg, q_seg, h_ref) = refs[3 * blocks:]
    ts, W = xa_ref.shape
    bw = W // blocks
    slabs = bw // LANES
    nseg = SUBLANES
    seg = ts // nseg
    pad = SUBLANES
    t = pl.program_id(0)
    cur = t % 2
    pa_ref = p_ref.at[cur]
    pb_ref = p_ref.at[1 - cur]
    ya_ref = y_ref.at[1 - cur]
    yb_ref = y_ref.at[cur]

    @pl.when(t == 0)
    def _():
        p_ref[...] = jnp.zeros_like(p_ref)
        y_ref[...] = jnp.zeros_like(y_ref)

    @pl.when((t == 0) | ((t - 1) % ns == 0))
    def _():
        buf_ref[0:pad, :] = jnp.zeros((pad, W), F32)
        h_ref[...] = jnp.zeros_like(h_ref)

    xn = _rms_rows(xa_ref[...], g_ref[...]).astype(BF16)
    yb = yb_ref[...]
    na = W // blocks
    nc = W // blocks
    row8 = lax.broadcasted_iota(jnp.int32, (SUBLANES, LANES), 0)
    buf_ref[pad:pad + ts, :] = pb_ref[:, W:].astype(F32)
    gate = pb_ref[:, :W]

    for blk in range(blocks):
        cs = slice(blk * bw, (blk + 1) * bw)
        par = blk % 2
        conv = cb_ref[:, cs] + cw_ref[conv_w - 1:conv_w, cs] * buf_ref[pad:pad + ts, cs]
        for tap in range(conv_w - 1):
            off = pad - (conv_w - 1) + tap
            conv = conv + cw_ref[tap:tap + 1, cs] * buf_ref[off:off + ts, cs]
        pre = jnp.dot(conv.astype(BF16), gw_ref[blk], preferred_element_type=F32) + gb_ref[blk]
        ca = slice(2 * blk * na, (2 * blk + 1) * na)
        pa_ref[:, ca] = jnp.dot(xn, win_refs[2 * blk][...], preferred_element_type=F32).astype(BF16)
        r = jax.nn.sigmoid(pre[:, :bw])
        i = jax.nn.sigmoid(pre[:, bw:])
        log_a = (R_C * jax.nn.log_sigmoid(ap_ref[:, cs])) * r
        a = jnp.exp(log_a)
        z = jnp.tanh(-log_a) * (a * a + 1.0)
        mult = jnp.where(z == 0.0, 0.0, z * lax.rsqrt(z))
        u = mult * (i * conv)
        ca = slice((2 * blk + 1) * na, (2 * blk + 2) * na)
        pa_ref[:, ca] = jnp.dot(xn, win_refs[2 * blk + 1][...], preferred_element_type=F32).astype(BF16)

        for sl in range(slabs):
            ls = slice(sl * LANES, (sl + 1) * LANES)
            lanes = slice(blk * bw + sl * LANES, blk * bw + (sl + 1) * LANES)
            a_s, u_s = a_seg.at[par, sl], u_seg.at[par, sl]
            h_s, q_s = h_seg.at[par, sl], q_seg.at[par, sl]
            for s in range(nseg):
                for i8 in range(seg // SUBLANES):
                    rows = slice(seg * s + SUBLANES * i8, seg * s + SUBLANES * (i8 + 1))
                    dst = pl.ds(SUBLANES * SUBLANES * i8 + s, SUBLANES, stride=SUBLANES)
                    a_s[dst, :] = a[rows, ls]
                    u_s[dst, :] = u[rows, ls]
            h = jnp.zeros((SUBLANES, LANES), F32)
            q = jnp.ones((SUBLANES, LANES), F32)
            for j in range(seg):
                rj = slice(SUBLANES * j, SUBLANES * (j + 1))
                aj = a_s[rj, :]
                h = aj * h + u_s[rj, :]
                q = aj * q
                h_s[rj, :] = h
                q_s[rj, :] = q
            h0 = h_ref[:, lanes]
            e = jnp.where(row8 == 0, q * h0 + h, h)
            d = 1
            while d < nseg:
                keep = row8 >= d
                e = jnp.where(keep, q * pltpu.roll(e, d, axis=0) + e, e)
                q = jnp.where(keep, q * pltpu.roll(q, d, axis=0), q)
                d *= 2
            h_ref[:, lanes] = e[nseg - 1:nseg, :]
            h_in = jnp.where(row8 == 0, h0, pltpu.roll(e, 1, axis=0))
            for j in range(seg):
                rj = slice(SUBLANES * j, SUBLANES * (j + 1))
                h_s[rj, :] = h_s[rj, :] + q_s[rj, :] * h_in
            h_nat = jnp.concatenate(
                [h_s[pl.ds(SUBLANES * SUBLANES * i8 + s, SUBLANES, stride=SUBLANES), :]
                 for s in range(nseg) for i8 in range(seg // SUBLANES)], axis=0)
            ya_ref[:, lanes] = (jax.nn.gelu(gate[:, lanes].astype(F32)) * h_nat).astype(BF16)

        cc = slice(blk * nc, (blk + 1) * nc)
        o_ref[:, cc] = xc_ref[:, cc] + jnp.dot(yb, wout_refs[blk][...], preferred_element_type=F32)
    buf_ref[0:pad, :] = buf_ref[ts:ts + pad, :]


def _rglru_layer(x, gain, w_in, conv_w, conv_b, gate_w, gate_b, a_param, w_out, *, batch, seq):
    T, D = x.shape
    ts = RGLRU_TILE
    blocks, W = gate_w.shape[1], w_out.shape[1]
    ns = seq // ts
    n = batch * ns
    kern = functools.partial(_rglru_layer_kernel, blocks=blocks, conv_w=conv_w.shape[0], ns=ns)
    const2 = lambda t: (0, 0)
    const3 = lambda t: (0, 0, 0)
    one = pl.Buffered(1)
    n_in, n_out = 2 * blocks, blocks
    w_in_specs = [pl.BlockSpec((None, D, 2 * W // n_in), lambda t, s=s: (0, 0, s), pipeline_mode=one)
                  for s in range(n_in)]
    w_out_specs = [pl.BlockSpec((None, W, D // n_out), lambda t, s=s: (0, 0, s), pipeline_mode=one)
                   for s in range(n_out)]
    seg_scratch = pltpu.VMEM((2, W // blocks // LANES, ts, LANES), F32)
    return pl.pallas_call(
        kern,
        grid=(n + 2,),
        in_specs=[pl.BlockSpec((ts, D), lambda t: (jnp.minimum(t, n - 1), 0)),
                  pl.BlockSpec((ts, D), lambda t: (jnp.clip(t - 2, 0, n - 1), 0)),
                  pl.BlockSpec((1, D), const2),
                  *w_in_specs,
                  *w_out_specs,
                  pl.BlockSpec(conv_w.shape, const2),
                  pl.BlockSpec((1, W), const2),
                  pl.BlockSpec((None,) + gate_w.shape[1:], lambda t: (0, 0, 0, 0), pipeline_mode=one),
                  pl.BlockSpec(gate_b.shape, const3),
                  pl.BlockSpec((1, W), const2)],
        out_specs=pl.BlockSpec((ts, D), lambda t: (jnp.clip(t - 2, 0, n - 1), 0)),
        out_shape=jax.ShapeDtypeStruct((T, D), F32),
        scratch_shapes=[pltpu.VMEM((2, ts, 2 * W), BF16),
                        pltpu.VMEM((2, ts, W), BF16),
                        pltpu.VMEM((ts + SUBLANES, W), F32),
                        seg_scratch, seg_scratch, seg_scratch, seg_scratch,
                        pltpu.VMEM((1, W), F32)],
        compiler_params=_cparams(("arbitrary",)),
        name="rglru_layer",
    )(x, x, gain, *([w_in] * n_in), *([w_out] * n_out), conv_w, conv_b, gate_w, gate_b, a_param)


def _pad_lanes(w):
    return jnp.pad(w, ((0, 0), (0, LANES - w.shape[1])))


def kernel(x, norm_mix, norm_ffn, norm_final, m_w_in, m_b_if, m_head_norm, m_w_out,
           r_w_in, r_conv_w, r_conv_b, r_gate_w, r_gate_b, r_a_param, r_w_out,
           ffn_w_in, ffn_w_out):
    B, S, D = x.shape
    T = B * S
    H = M_HEADS
    v_dim = m_w_out.shape[1]
    qk_dim = (m_w_in.shape[2] - 2 * v_dim - 2 * H) // 2
    n_main = 2 * qk_dim + 2 * v_dim
    row = lambda v: v.reshape(1, -1)

    h = x.reshape(T, D)

    ffn_w_in_b = ffn_w_in.astype(BF16)
    ffn_w_out_b = ffn_w_out.astype(BF16)

    w_if = m_w_in[0, :, n_main:]
    w_gates = jnp.concatenate([_pad_lanes(w_if[:, :H]), _pad_lanes(w_if[:, H:])], axis=1).astype(BF16)
    b_i = _pad_lanes(row(m_b_if[0, :H]))
    b_f = _pad_lanes(row(m_b_if[0, H:]))
    mixed = _mlstm_layer(h, row(norm_mix[0]), m_w_in.astype(BF16), w_gates, b_i, b_f, row(m_head_norm[0]),
                         batch=B, seq=S, heads=H, dk=qk_dim // H, dv=v_dim // H)
    h = _proj_residual(mixed, m_w_out.astype(BF16), 0, h)
    h = _ffn(h, row(norm_ffn[0]), ffn_w_in_b, ffn_w_out_b, 0, row(norm_final), final_norm=False)

    gate_b = r_gate_b[0].reshape(R_BLOCKS, 1, -1)
    h = _rglru_layer(h, row(norm_mix[1]), r_w_in.astype(BF16), r_conv_w[0], row(r_conv_b[0]),
                     r_gate_w.astype(BF16), gate_b, row(r_a_param[0]), r_w_out.astype(BF16), batch=B, seq=S)
    h = _ffn(h, row(norm_ffn[1]), ffn_w_in_b, ffn_w_out_b, 1, row(norm_final), final_norm=True)
    return h.reshape(B, S, D)
```
